```python
import math
import jax
import jax.numpy as jnp
from jax import lax
import numpy as np

D_MODEL = 1024
BATCH = 8
SEQ = 4096
DEPTH = 4

CTX_LEN = 256
GRID_W = 64
N_MIXERS = 3
CHUNK = 128
N_RET = (DEPTH + 2) // 3
N_MLSTM = (DEPTH + 1) // 3
N_RWKV = DEPTH // 3

RET_HEADS = 4
RET_DK = D_MODEL // RET_HEADS
RET_QK = RET_HEADS * RET_DK
RET_V = 2 * D_MODEL
RET_DV = RET_V // RET_HEADS
RET_IN = 2 * RET_QK + 2 * RET_V
ROPE_BASE = 10000.0

MLSTM_INNER = 2 * D_MODEL
MLSTM_HEADS = 4
MLSTM_DH = MLSTM_INNER // MLSTM_HEADS
QKV_BLOCK = 4
N_QKV_BLOCKS = MLSTM_INNER // QKV_BLOCK
MLSTM_CONV = 3

RWKV_N = 64
RWKV_HEADS = D_MODEL // RWKV_N
RWKV_DECAY_LORA = 64
RWKV_A_LORA = 64

DEEPNORM_ALPHA = (2.0 * DEPTH) ** 0.25
DEEPNORM_BETA = (8.0 * DEPTH) ** -0.25
LN_EPS = 1e-5
HEAD_NORM_EPS = 1e-6
RWKV_GN_EPS = 64e-5

kernel_name = "hybrid_retnet_mlstm_rwkv7_deepnorm_dit"


def _layer_norm(x, g, b):
    xf = x.astype(jnp.float32)
    mu = xf.mean(-1, keepdims=True)
    var = jnp.mean(jnp.square(xf - mu), -1, keepdims=True)
    return ((xf - mu) * lax.rsqrt(var + LN_EPS) * g.astype(jnp.float32) + b.astype(jnp.float32)).astype(x.dtype)


def _chunks(t):
    B, H, T = t.shape[:3]
    return jnp.moveaxis(t.reshape(B, H, T // CHUNK, CHUNK, *t.shape[3:]), 2, 0)


def _unchunks(t):
    t = jnp.moveaxis(t, 0, 2)
    return t.reshape(t.shape[0], t.shape[1], -1, *t.shape[4:])


def _rope_2d(x, rows, cols):
    half = x.shape[-1] // 2
    nf = half // 2
    inv = ROPE_BASE ** (-jnp.arange(nf, dtype=jnp.float32) / nf)

    def rot(xa, pos):
        ang = pos[:, None] * inv[None, :]
        cos, sin = jnp.cos(ang), jnp.sin(ang)
        x1, x2 = xa[..., :nf], xa[..., nf:]
        return jnp.concatenate([x1 * cos - x2 * sin, x1 * sin + x2 * cos], -1)

    return jnp.concatenate([rot(x[..., :half], rows), rot(x[..., half:], cols)], -1)


def _retention_scan(q, k, v, log_gamma, state0):
    idx = jnp.arange(CHUNK, dtype=jnp.float32)
    lg = log_gamma[:, None]
    diff = idx[:, None] - idx[None, :]
    decay_mask = jnp.where(diff >= 0, jnp.exp(diff * lg[:, :, None]), 0.0)
    q_decay = jnp.exp((idx + 1.0) * lg)[:, :, None]
    k_decay = jnp.exp((CHUNK - 1.0 - idx) * lg)[:, :, None]
    chunk_decay = jnp.exp(CHUNK * log_gamma)[:, None, None]

    def step(S, inp):
        qc, kc, vc = inp
        s = jnp.einsum('bhid,bhjd->bhij', qc, kc) * decay_mask
        o = jnp.einsum('bhij,bhjv->bhiv', s, vc) + jnp.einsum('bhid,bhdv->bhiv', qc * q_decay, S)
        S = S * chunk_decay + jnp.einsum('bhjd,bhjv->bhdv', kc * k_decay, vc)
        return S, o

    S, o = lax.scan(step, state0, (_chunks(q), _chunks(k), _chunks(v)))
    return _unchunks(o), S


def _retention_mixer(h, hc, w_in, decay_logit, w_out, ctx_out):
    f32 = jnp.float32
    B, T, _ = h.shape

    def project(u):
        p = (u @ w_in).astype(f32)
        q, k, v, g = jnp.split(p, [RET_QK, 2 * RET_QK, 2 * RET_QK + RET_V], axis=-1)
        heads = lambda t, d: t.reshape(t.shape[0], t.shape[1], RET_HEADS, d).transpose(0, 2, 1, 3)
        return heads(q, RET_DK), heads(k, RET_DK) * RET_DK ** -0.5, heads(v, RET_DV), g

    q, k, v, g = project(h)
    qc, kc, vc, gc = project(hc)
    t = jnp.arange(T)
    rows = (t // GRID_W).astype(f32)
    cols = (t % GRID_W).astype(f32)
    q = _rope_2d(q, rows, cols)
    k = _rope_2d(k, rows, cols)
    log_gamma = jax.nn.log_sigmoid(decay_logit.astype(f32))
    zero = jnp.zeros((B, RET_HEADS, RET_DK, RET_DV), f32)

    def direction(d, rev):
        f = (lambda a: jnp.flip(a, axis=2)) if rev else (lambda a: a)
        oc_d, state = _retention_scan(f(qc), f(kc), f(vc), log_gamma[d], zero)
        o_d, _ = _retention_scan(f(q), f(k), f(v), log_gamma[d], state)
        return f(o_d), f(oc_d)

    o_f, oc_f = direction(0, False)
    o_b, oc_b = direction(1, True)

    def finish(o, gate):
        o = o * lax.rsqrt(jnp.mean(o * o, -1, keepdims=True) + HEAD_NORM_EPS)
        o = o.transpose(0, 2, 1, 3).reshape(o.shape[0], o.shape[2], RET_V)
        return (jax.nn.silu(gate) * o).astype(h.dtype) @ w_out

    y = finish(o_f + o_b, g)
    yc = finish(oc_f + oc_b, gc) if ctx_out else None
    return y, yc


def _centred_dwconv(u, w, b):
    K = w.shape[0]
    p = K // 2
    T = u.shape[1]
    up = jnp.pad(u, ((0, 0), (p, p), (0, 0)))
    out = b + up[:, 0:T] * w[0]
    for j in range(1, K):
        out = out + up[:, j:j + T] * w[j]
    return out


def _headwise(u, w):
    B, T, _ = u.shape
    return jnp.einsum('btnc,ncd->btnd', u.reshape(B, T, N_QKV_BLOCKS, QKV_BLOCK), w).reshape(B, T, MLSTM_INNER)


def _mlstm_scan(q, k, v, i_pre, logf, state0):
    tril = jnp.tril(jnp.ones((CHUNK, CHUNK), dtype=bool))

    def step(carry, inp):
        C, n, m = carry
        qc, kc, vc, ic, fc = inp
        b = jnp.cumsum(fc, axis=-1)
        a = b + m[..., None]
        dlog = jnp.where(tril, b[..., :, None] - b[..., None, :] + ic[..., None, :], -jnp.inf)
        m_t = jnp.maximum(a, dlog.max(-1))
        s = jnp.einsum('bhid,bhjd->bhij', qc, kc) * jnp.exp(dlog - m_t[..., None])
        inter = jnp.exp(a - m_t)
        num = jnp.einsum('bhij,bhjv->bhiv', s, vc) + inter[..., None] * jnp.einsum('bhid,bhdv->bhiv', qc, C)
        den = s.sum(-1) + inter * jnp.einsum('bhid,bhd->bhi', qc, n)
        hout = num / jnp.maximum(jnp.abs(den), jnp.exp(-m_t))[..., None]
        bl = b[..., -1]
        wlog = bl[..., None] - b + ic
        m_new = jnp.maximum(bl + m, wlog.max(-1))
        wk = kc * jnp.exp(wlog - m_new[..., None])[..., None]
        decay = jnp.exp(bl + m - m_new)
        C = decay[..., None, None] * C + jnp.einsum('bhjd,bhjv->bhdv', wk, vc)
        n = decay[..., None] * n + wk.sum(2)
        return (C, n, m_new), hout

    carry, hs = lax.scan(step, state0, (_chunks(q), _chunks(k), _chunks(v), _chunks(i_pre), _chunks(logf)))
    return _unchunks(hs), carry


def _mlstm_mixer(h, hc, w_in, conv_w, conv_b, w_qkv, gate_w, gate_b, skip, gn_g, w_out, ctx_out):
    f32 = jnp.float32
    B = h.shape[0]
    heads = lambda t: t.reshape(t.shape[0], t.shape[1], MLSTM_HEADS, MLSTM_DH).transpose(0, 2, 1, 3).astype(f32)

    def prep(u):
        xm, z = jnp.split(u @ w_in, 2, axis=-1)
        xconv = jax.nn.silu(_centred_dwconv(xm, conv_w, conv_b))
        q = _headwise(xconv, w_qkv[0])
        k = _headwise(xconv, w_qkv[1])
        v = _headwise(xm, w_qkv[2])
        qkv = jnp.concatenate([q, k, v], axis=-1)
        gates = [jnp.moveaxis((qkv @ gate_w[d] + gate_b[d]).astype(f32), -1, 1) for d in range(2)]
        ig = [gd[:, :MLSTM_HEADS] for gd in gates]
        lf = [jax.nn.log_sigmoid(gd[:, MLSTM_HEADS:]) for gd in gates]
        return heads(q), heads(k) * MLSTM_DH ** -0.5, heads(v), ig, lf, xconv, z

    q, k, v, ig, lf, xconv, z = prep(h)
    qc, kc, vc, igc, lfc, xconvc, zc = prep(hc)
    zero = (jnp.zeros((B, MLSTM_HEADS, MLSTM_DH, MLSTM_DH), f32),
            jnp.zeros((B, MLSTM_HEADS, MLSTM_DH), f32),
            jnp.zeros((B, MLSTM_HEADS), f32))

    def direction(d, rev):
        f = (lambda a: jnp.flip(a, axis=2)) if rev else (lambda a: a)
        oc_d, state = _mlstm_scan(f(qc), f(kc), f(vc), f(igc[d]), f(lfc[d]), zero)
        o_d, _ = _mlstm_scan(f(q), f(k), f(v), f(ig[d]), f(lf[d]), state)
        return f(o_d), f(oc_d)

    o_f, oc_f = direction(0, False)
    o_b, oc_b = direction(1, True)

    def finish(o, xcv, zz):
        Bn, _, Tn, _ = o.shape
        mu = o.mean(-1, keepdims=True)
        var = jnp.mean(jnp.square(o - mu), -1, keepdims=True)
        o = ((o - mu) * lax.rsqrt(var + LN_EPS)).transpose(0, 2, 1, 3).reshape(Bn, Tn, MLSTM_INNER) * gn_g
        o = (o + skip * xcv.astype(f32)) * jax.nn.silu(zz.astype(f32))
        return o.astype(h.dtype) @ w_out

    y = finish(o_f + o_b, xconv, z)
    yc = finish(oc_f + oc_b, xconvc, zc) if ctx_out else None
    return y, yc


def _qshift_grid(u):
    B, T, D = u.shape
    g = u.reshape(B, T // GRID_W, GRID_W, D)
    q = D // 4
    left = jnp.pad(g[:, :, :-1, :q], ((0, 0), (0, 0), (1, 0), (0, 0)))
    right = jnp.pad(g[:, :, 1:, q:2 * q], ((0, 0), (0, 0), (0, 1), (0, 0)))
    up = jnp.pad(g[:, :-1, :, 2 * q:3 * q], ((0, 0), (1, 0), (0, 0), (0, 0)))
    down = jnp.pad(g[:, 1:, :, 3 * q:], ((0, 0), (0, 1), (0, 0), (0, 0)))
    return jnp.concatenate([left, right, up, down], -1).reshape(B, T, D)


def _shift_seq(u):
    hd = u.shape[-1] // 2
    prev = jnp.pad(u[:, :-1, :hd], ((0, 0), (1, 0), (0, 0)))
    nxt = jnp.pad(u[:, 1:, hd:], ((0, 0), (0, 1), (0, 0)))
    return jnp.concatenate([prev, nxt], -1)


def _rwkv_scan(r, w, k, v, a, b, state0):
    xs = tuple(jnp.moveaxis(t, 1, 0) for t in (r, w, k, v, a, b))

    def step(S, inp):
        rt, wt, kt, vt, at, bt = inp
        sa = jnp.einsum('bhij,bhj->bhi', S, at)
        S = S * wt[:, :, None, :] + sa[..., None] * bt[:, :, None, :] + vt[..., None] * kt[:, :, None, :]
        return S, jnp.einsum('bhij,bhj->bhi', S, rt)

    S, ys = lax.scan(step, state0, xs)
    return jnp.moveaxis(ys, 0, 1), S


def _rwkv7_mixer(h, hc, mix, w_rkvg, w0, w1, w2, a0, a1, a2, k_k, k_a, r_k, gn_g, gn_b, w_out, ctx_out):
    f32 = jnp.float32
    B = h.shape[0]
    heads = lambda t: t.reshape(t.shape[0], t.shape[1], RWKV_HEADS, RWKV_N).astype(f32)

    def prep(u, shifted):
        xx = shifted - u
        xr, xw, xk, xv, xa, xg = (u + xx * mix[j] for j in range(6))
        r, k, v, g = jnp.einsum('nbtd,nde->nbte', jnp.stack([xr, xk, xv, xg]), w_rkvg)
        kk = heads(k * k_k)
        kk = kk / jnp.maximum(jnp.sqrt(jnp.sum(kk * kk, -1, keepdims=True)), 1e-12)
        per_dir = []
        for d in range(2):
            wlog = -jax.nn.softplus(-(w0[d] + jnp.tanh(xw @ w1[d]) @ w2[d]).astype(f32)) - 0.5
            ad = jax.nn.sigmoid((a0[d] + (xa @ a1[d]) @ a2[d]).astype(f32))
            kd = k.astype(f32) * (1.0 + (ad - 1.0) * k_a.astype(f32))
            per_dir.append((heads(jnp.exp(-jnp.exp(wlog))), heads(kd), -kk, kk * heads(ad)))
        return heads(r), heads(v), g, per_dir

    r, v, g, dirs = prep(h, _qshift_grid(h))
    rc, vc, gc, dirs_c = prep(hc, _shift_seq(hc))
    zero = jnp.zeros((B, RWKV_HEADS, RWKV_N, RWKV_N), f32)

    def direction(d, rev):
        f = (lambda t: jnp.flip(t, axis=1)) if rev else (lambda t: t)
        wc_, kc_, ac_, bc_ = dirs_c[d]
        yc_d, state = _rwkv_scan(f(rc), f(wc_), f(kc_), f(vc), f(ac_), f(bc_), zero)
        w_, k_, a_, b_ = dirs[d]
        y_d, _ = _rwkv_scan(f(r), f(w_), f(k_), f(v), f(a_), f(b_), state)
        return f(y_d), f(yc_d)

    y_f, yc_f = direction(0, False)
    y_b, yc_b = direction(1, True)

    def finish(y, r_, v_, k_fwd, k_bwd, gate):
        Bn, Tn = y.shape[:2]
        mu = y.mean(-1, keepdims=True)
        var = jnp.mean(jnp.square(y - mu), -1, keepdims=True)
        o = ((y - mu) * lax.rsqrt(var + RWKV_GN_EPS)).reshape(Bn, Tn, D_MODEL) * gn_g + gn_b
        bonus = ((r_ * k_fwd * r_k).sum(-1, keepdims=True) * v_
                 + (r_ * k_bwd * r_k).sum(-1, keepdims=True) * v_).reshape(Bn, Tn, D_MODEL)
        o = (o + bonus) * jax.nn.silu(gate.astype(f32))
        return o.astype(h.dtype) @ w_out

    y = finish(y_f + y_b, r, v, dirs[0][1], dirs[1][1], g)
    yc = finish(yc_f + yc_b, rc, vc, dirs_c[0][1], dirs_c[1][1], gc) if ctx_out else None
    return y, yc


def setup_inputs(seed: int = 0) -> dict:
    key = jax.random.key(seed)
    ks = iter(jax.random.split(key, 48))
    nrm = lambda shape, s: jax.random.normal(next(ks), shape, jnp.float32) * s
    D, I, H = D_MODEL, MLSTM_INNER, MLSTM_HEADS

    gamma0 = 1.0 - 2.0 ** (-5.0 - np.arange(RET_HEADS))
    ret_decay0 = jnp.asarray(np.log(gamma0 / (1.0 - gamma0)), jnp.float32)

    return {
        "x": nrm((BATCH, SEQ, D), 1.0),
        "c": nrm((BATCH, D), 1.0),
        "ctx": nrm((BATCH, CTX_LEN, D), 1.0),
        "c_ctx": nrm((D,), 1.0),
        "ada_w": nrm((DEPTH, D, 3 * D), D ** -0.5),
        "ada_b": nrm((DEPTH, 3 * D), 0.02),
        "ln_g": 1.0 + nrm((DEPTH, D), 0.02),
        "ln_b": nrm((DEPTH, D), 0.02),
        "ret_w_in": nrm((N_RET, D, RET_IN), D ** -0.5),
        "ret_decay": ret_decay0 + nrm((N_RET, 2, RET_HEADS), 0.05),
        "ret_w_out": nrm((N_RET, RET_V, D), RET_V ** -0.5 * DEEPNORM_BETA),
        "ml_w_in": nrm((N_MLSTM, D, 2 * I), D ** -0.5),
        "ml_conv_w": nrm((N_MLSTM, MLSTM_CONV, I), MLSTM_CONV ** -0.5),
        "ml_conv_b": nrm((N_MLSTM, I), 0.02),
        "ml_w_qkv": nrm((N_MLSTM, 3, N_QKV_BLOCKS, QKV_BLOCK, QKV_BLOCK), QKV_BLOCK ** -0.5),
        "ml_gate_w": nrm((N_MLSTM, 2, 3 * I, 2 * H), 0.1 * (3 * I) ** -0.5),
        "ml_gate_b": jnp.concatenate([nrm((N_MLSTM, 2, H), 0.1),
                                      jnp.linspace(3.0, 6.0, H) + nrm((N_MLSTM, 2, H), 0.05)], axis=-1),
        "ml_skip": 1.0 + nrm((N_MLSTM, I), 0.02),
        "ml_gn_g": 1.0 + nrm((N_MLSTM, I), 0.02),
        "ml_w_out": nrm((N_MLSTM, I, D), I ** -0.5 * DEEPNORM_BETA),
        "rk_mix": jax.random.uniform(next(ks), (N_RWKV, 6, D), jnp.float32, 0.2, 0.8),
        "rk_w_rkvg": nrm((N_RWKV, 4, D, D), D ** -0.5),
        "rk_w0": jnp.linspace(-6.0, -1.0, D) + nrm((N_RWKV, 2, D), 0.1),
        "rk_w1": nrm((N_RWKV, 2, D, RWKV_DECAY_LORA), 0.1 * D ** -0.5),
        "rk_w2": nrm((N_RWKV, 2, RWKV_DECAY_LORA, D), 0.1 * RWKV_DECAY_LORA ** -0.5),
        "rk_a0": nrm((N_RWKV, 2, D), 0.1),
        "rk_a1": nrm((N_RWKV, 2, D, RWKV_A_LORA), 0.1 * D ** -0.5),
        "rk_a2": nrm((N_RWKV, 2, RWKV_A_LORA, D), 0.1 * RWKV_A_LORA ** -0.5),
        "rk_k_k": 0.85 + nrm((N_RWKV, D), 0.02),
        "rk_k_a": 1.0 + nrm((N_RWKV, D), 0.02),
        "rk_r_k": nrm((N_RWKV, RWKV_HEADS, RWKV_N), 0.1),
        "rk_gn_g": 1.0 + nrm((N_RWKV, D), 0.02),
        "rk_gn_b": nrm((N_RWKV, D), 0.02),
        "rk_w_out": nrm((N_RWKV, D, D), D ** -0.5 * DEEPNORM_BETA),
    }


def reference(x, c, ctx, c_ctx, ada_w, ada_b, ln_g, ln_b,
              ret_w_in, ret_decay, ret_w_out,
              ml_w_in, ml_conv_w, ml_conv_b, ml_w_qkv, ml_gate_w, ml_gate_b, ml_skip, ml_gn_g, ml_w_out,
              rk_mix, rk_w_rkvg, rk_w0, rk_w1, rk_w2, rk_a0, rk_a1, rk_a2, rk_k_k, rk_k_a, rk_r_k,
              rk_gn_g, rk_gn_b, rk_w_out):
    sc = jax.nn.silu(c)
    scc = jax.nn.silu(c_ctx)
    xc = ctx
    for i in range(DEPTH):
        last = i == DEPTH - 1
        mod = sc @ ada_w[i] + ada_b[i]
        modc = scc @ ada_w[i] + ada_b[i]
        shift, scale, gate = jnp.split(mod[:, None, :], 3, axis=-1)
        shift_c, scale_c, gate_c = jnp.split(modc, 3)
        h = x * (1.0 + scale) + shift
        hc = xc * (1.0 + scale_c) + shift_c
        kind, j = i % N_MIXERS, i // N_MIXERS
        if kind == 0:
            y, yc = _retention_mixer(h, hc, ret_w_in[j], ret_decay[j], ret_w_out[j], not last)
        elif kind == 1:
            y, yc = _mlstm_mixer(h, hc, ml_w_in[j], ml_conv_w[j], ml_conv_b[j], ml_w_qkv[j], ml_gate_w[j],
                                 ml_gate_b[j], ml_skip[j], ml_gn_g[j], ml_w_out[j], not last)
        else:
            y, yc = _rwkv7_mixer(h, hc, rk_mix[j], rk_w_rkvg[j], rk_w0[j], rk_w1[j], rk_w2[j], rk_a0[j],
                                 rk_a1[j], rk_a2[j], rk_k_k[j], rk_k_a[j], rk_r_k[j], rk_gn_g[j], rk_gn_b[j],
                                 rk_w_out[j], not last)
        x = _layer_norm(DEEPNORM_ALPHA * x + gate * y, ln_g[i], ln_b[i])
        if not last:
            xc = _layer_norm(DEEPNORM_ALPHA * xc + gate_c * yc, ln_g[i], ln_b[i])
    return x
```

```python
import functools

import jax
import jax.numpy as jnp
from jax import lax
from jax.experimental import pallas as pl
from jax.experimental.pallas import tpu as pltpu

F32 = jnp.float32
BF16 = jnp.bfloat16

GRID_W = 64
RET_HEADS = 4
MLSTM_HEADS = 4
QKV_BLOCK = 4
RWKV_N = 64
ROPE_BASE = 10000.0
LN_EPS = 1e-5
HEAD_NORM_EPS = 1e-6
RWKV_GN_EPS = 64e-5

LANES = 128
ROW_TILE = 256
RET_CHUNK = 128
ML_CHUNK = 128
RK_CHUNK = 64
RK_PAIR = 2
VMEM_LIMIT = 60 * 1024 * 1024


def _cparams(n_axes):
    return pltpu.CompilerParams(dimension_semantics=("arbitrary",) * n_axes, vmem_limit_bytes=VMEM_LIMIT)


def _silu(x):
    return x * jax.nn.sigmoid(x)


def _dot(a, b):
    return jnp.dot(a, b, preferred_element_type=F32)


def _dot_nt(a, b):
    return lax.dot_general(a, b, (((1,), (1,)), ((), ())), preferred_element_type=F32)


def _dot_tn(a, b):
    return lax.dot_general(a, b, (((0,), (0,)), ((), ())), preferred_element_type=F32)


def _split2(a):
    hi = a.astype(BF16)
    lo = (a - hi.astype(F32)).astype(BF16)
    return hi, lo


def _dot_hi(a, b):
    ah, al = _split2(a)
    bh, bl = _split2(b)
    return _dot(ah, bh) + (_dot(ah, bl) + _dot(al, bh))


def _const_spec(shape):
    nd = len(shape)
    return pl.BlockSpec(shape, lambda *_: (0,) * nd, pipeline_mode=pl.Buffered(1))


def _modulate(x, ms):
    return x * (1.0 + ms[1:2]) + ms[0:1]


def _chunk_of(idx, rev, n_ctx, n_all):
    if not rev:
        return idx
    return jnp.where(idx < n_ctx, n_ctx - 1 - idx, n_all - 1 + n_ctx - idx)


def _ada_kernel(c_ref, w_ref, b_ref, o_ref):
    sc = _silu(c_ref[...])
    o_ref[0] = _dot(sc.astype(BF16), w_ref[0].astype(BF16)) + b_ref[0]


def _ada_mods(c_all, ada_w, ada_b):
    depth, d, _ = ada_w.shape
    rows = c_all.shape[0]
    return pl.pallas_call(
        _ada_kernel,
        grid=(depth, 3),
        in_specs=[
            pl.BlockSpec((rows, d), lambda l, j: (0, 0)),
            pl.BlockSpec((1, d, d), lambda l, j: (l, 0, j)),
            pl.BlockSpec((1, 1, d), lambda l, j: (l, 0, j)),
        ],
        out_specs=pl.BlockSpec((1, rows, d), lambda l, j: (l, 0, j)),
        out_shape=jax.ShapeDtypeStruct((depth, rows, 3 * d), F32),
        compiler_params=_cparams(2),
        name="ada_mods",
    )(c_all, ada_w, ada_b.reshape(depth, 1, 3 * d))


def _out_kernel(*refs, mode, alpha):
    if mode == "ret":
        o_ref, g_ref, w_ref, x_ref, ms_ref, lng_ref, lnb_ref, out_ref = refs
        a = _silu(g_ref[0]) * o_ref[0]
    elif mode == "ml":
        o_ref, xc_ref, z_ref, gn_ref, sk_ref, w_ref, x_ref, ms_ref, lng_ref, lnb_ref, out_ref = refs
        a = (o_ref[0] * gn_ref[...] + sk_ref[...] * xc_ref[0]) * _silu(z_ref[0])
    else:
        o_ref, g_ref, w_ref, x_ref, ms_ref, lng_ref, lnb_ref, out_ref = refs
        a = o_ref[0] * _silu(g_ref[0])
    y = _dot(a.astype(BF16), w_ref[...])
    gate = ms_ref[0, 0][2:3]
    z = alpha * x_ref[0] + gate * y
    mu = jnp.mean(z, -1, keepdims=True)
    zc = z - mu
    var = jnp.mean(zc * zc, -1, keepdims=True)
    out_ref[0] = zc * lax.rsqrt(var + LN_EPS) * lng_ref[...] + lnb_ref[...]


def _out_call(mode, acts, vecs, w_out, xs, msel, ln_g, ln_b, alpha):
    bsz, tt, d = xs.shape
    n_tiles = tt // ROW_TILE
    inner = w_out.shape[0]
    tile = lambda width: pl.BlockSpec((1, ROW_TILE, width), lambda b, t: (b, t, 0))
    in_specs = [tile(inner) for _ in acts] + [_const_spec((1, inner)) for _ in vecs] + [
        _const_spec((inner, d)),
        tile(d),
        pl.BlockSpec((1, 1, 3, d), lambda b, t: (b, jnp.minimum(t, 1), 0, 0)),
        _const_spec((1, d)),
        _const_spec((1, d)),
    ]
    return pl.pallas_call(
        functools.partial(_out_kernel, mode=mode, alpha=alpha),
        grid=(bsz, n_tiles),
        in_specs=in_specs,
        out_specs=tile(d),
        out_shape=jax.ShapeDtypeStruct((bsz, tt, d), F32),
        compiler_params=_cparams(2),
        name=f"{mode}_out",
    )(*acts, *[v.reshape(1, inner) for v in vecs], w_out.astype(BF16), xs, msel,
      ln_g.reshape(1, d), ln_b.reshape(1, d))


def _ret_in_kernel(x_ref, ms_ref, w_ref, cos_ref, sin_ref, qkv_ref, g_ref, *, dk, qk, vw):
    hm = _modulate(x_ref[0], ms_ref[0, 0]).astype(BF16)
    cos = cos_ref[...]
    sin = sin_ref[...]
    for j in range(2 * qk // dk):
        p = _dot(hm, w_ref[:, j * dk:(j + 1) * dk])
        pr = jnp.concatenate(
            [pltpu.roll(p[:, s * LANES:(s + 1) * LANES], LANES // 2, 1) for s in range(dk // LANES)], axis=1)
        p = p * cos + pr * sin
        if j * dk >= qk:
            p = p * dk ** -0.5
        qkv_ref[0, :, j * dk:(j + 1) * dk] = p.astype(BF16)
    qkv_ref[0, :, 2 * qk:] = _dot(hm, w_ref[:, 2 * qk:2 * qk + vw]).astype(BF16)
    g_ref[0] = _dot(hm, w_ref[:, 2 * qk + vw:])


def _ret_scan_kernel(dec_ref, q_ref, k_ref, v_ref, o_ref, s_ref, *, n_ctx, n_all):
    h = pl.program_id(1)
    L = RET_CHUNK
    ii = lax.broadcasted_iota(jnp.int32, (L, L), 0)
    jj = lax.broadcasted_iota(jnp.int32, (L, L), 1)
    pos = lax.broadcasted_iota(jnp.int32, (L, 1), 0).astype(F32)

    def run(d, rev):
        lg = jax.nn.log_sigmoid(jnp.full((1, 1), dec_ref[d, h], F32))
        dist = (jj - ii) if rev else (ii - jj)
        mask = jnp.where(dist >= 0, jnp.exp(jnp.maximum(dist, 0).astype(F32) * lg), 0.0)
        if rev:
            q_decay = jnp.exp((L - pos) * lg)
            k_decay = jnp.exp(pos * lg)
        else:
            q_decay = jnp.exp((pos + 1.0) * lg)
            k_decay = jnp.exp((L - 1.0 - pos) * lg)
        chunk_decay = jnp.exp(L * lg)
        s_ref[...] = jnp.zeros_like(s_ref)

        def body(idx, carry):
            c = _chunk_of(idx, rev, n_ctx, n_all)
            rows = pl.ds(pl.multiple_of(c * L, L), L)
            qc = q_ref[0, rows, :]
            kc = k_ref[0, rows, :]
            vc = v_ref[0, rows, :]
            s = _dot_nt(qc, kc) * mask
            o = _dot(s.astype(BF16), vc) + q_decay * _dot(qc, s_ref[...].astype(BF16))
            kdec = (kc.astype(F32) * k_decay).astype(BF16)
            s_ref[...] = s_ref[...] * chunk_decay + _dot_tn(kdec, vc)
            if rev:
                o = o + o_ref[0, rows, :]
                o = o * lax.rsqrt(jnp.mean(o * o, -1, keepdims=True) + HEAD_NORM_EPS)
            o_ref[0, rows, :] = o
            return carry

        lax.fori_loop(0, n_all, body, 0)

    run(0, False)
    run(1, True)


def _retention_layer(xs, msel, ln_g, ln_b, w_in, decay, w_out, rope_cos, rope_sin, alpha, ctx_len):
    bsz, tt, d = xs.shape
    n_tiles = tt // ROW_TILE
    heads = RET_HEADS
    dk = d // heads
    qk = d
    vw = w_out.shape[0]
    dv = vw // heads
    tile = lambda width: pl.BlockSpec((1, ROW_TILE, width), lambda b, t: (b, t, 0))
    qkv, g = pl.pallas_call(
        functools.partial(_ret_in_kernel, dk=dk, qk=qk, vw=vw),
        grid=(bsz, n_tiles),
        in_specs=[
            tile(d),
            pl.BlockSpec((1, 1, 3, d), lambda b, t: (b, jnp.minimum(t, 1), 0, 0)),
            _const_spec(w_in.shape),
            pl.BlockSpec((ROW_TILE, dk), lambda b, t: (t, 0)),
            pl.BlockSpec((ROW_TILE, dk), lambda b, t: (t, 0)),
        ],
        out_specs=[tile(2 * qk + vw), tile(vw)],
        out_shape=[jax.ShapeDtypeStruct((bsz, tt, 2 * qk + vw), BF16),
                   jax.ShapeDtypeStruct((bsz, tt, vw), F32)],
        compiler_params=_cparams(2),
        name="ret_in",
    )(xs, msel, w_in.astype(BF16), rope_cos, rope_sin)

    o = pl.pallas_call(
        functools.partial(_ret_scan_kernel, n_ctx=ctx_len // RET_CHUNK, n_all=tt // RET_CHUNK),
        grid=(bsz, heads),
        in_specs=[
            pl.BlockSpec(memory_space=pltpu.SMEM),
            pl.BlockSpec((1, tt, dk), lambda b, h: (b, 0, h)),
            pl.BlockSpec((1, tt, dk), lambda b, h: (b, 0, qk // dk + h)),
            pl.BlockSpec((1, tt, dv), lambda b, h: (b, 0, 2 * qk // dv + h)),
        ],
        out_specs=pl.BlockSpec((1, tt, dv), lambda b, h: (b, 0, h)),
        out_shape=jax.ShapeDtypeStruct((bsz, tt, vw), F32),
        scratch_shapes=[pltpu.VMEM((dk, dv), F32)],
        compiler_params=_cparams(2),
        name="ret_scan",
    )(decay.astype(F32), qkv, qkv, qkv)

    return _out_call("ret", [o, g], [], w_out, xs, msel, ln_g, ln_b, alpha)


def _ml_in_kernel(x_ref, xp_ref, xn_ref, ms_ref, w_ref, cw_ref, cb_ref, wbd_ref, gw_ref, gwt_ref, gb_ref,
                  gbt_ref, qkv_ref, xc_ref, z_ref, gc_ref, gr_ref, *, inner, n_tiles):
    t = pl.program_id(1)
    tm = x_ref.shape[1]
    ms = ms_ref[0, 0]
    hm = _modulate(x_ref[0], ms).astype(BF16)
    xm = _dot(hm, w_ref[:, :inner])
    z_ref[0] = _dot(hm, w_ref[:, inner:])
    hp = _modulate(xp_ref[0], ms).astype(BF16)
    hn = _modulate(xn_ref[0], ms).astype(BF16)
    prev_ok = (t >= 2).astype(F32)
    next_ok = jnp.logical_and(t >= 1, t < n_tiles - 1).astype(F32)
    xm_prev = _dot(hp, w_ref[:, :inner])[hp.shape[0] - 1:] * prev_ok
    xm_next = _dot(hn, w_ref[:, :inner])[0:1] * next_ok
    row = lax.broadcasted_iota(jnp.int32, (tm, 1), 0)
    x_dn = jnp.where(row == 0, xm_prev, pltpu.roll(xm, 1, 0))
    x_up = jnp.where(row == tm - 1, xm_next, pltpu.roll(xm, tm - 1, 0))
    cw = cw_ref[...]
    xc = _silu(cb_ref[...] + x_dn * cw[0:1] + xm * cw[1:2] + x_up * cw[2:3])
    xc_ref[0] = xc
    xcb = xc.astype(BF16)
    xmb = xm.astype(BF16)
    for gi in range(inner // LANES):
        sl = slice(gi * LANES, (gi + 1) * LANES)
        qkv_ref[0, :, gi * LANES:(gi + 1) * LANES] = _dot(xcb[:, sl], wbd_ref[0, gi]).astype(BF16)
        qkv_ref[0, :, inner + gi * LANES:inner + (gi + 1) * LANES] = _dot(xcb[:, sl], wbd_ref[1, gi]).astype(BF16)
        qkv_ref[0, :, 2 * inner + gi * LANES:2 * inner + (gi + 1) * LANES] = (
            _dot(xmb[:, sl], wbd_ref[2, gi]).astype(BF16))
    qkv = qkv_ref[0]
    gc_ref[0] = _dot(qkv, gw_ref[...]) + gb_ref[...]
    gr_ref[0] = _dot_nt(gwt_ref[...], qkv) + gbt_ref[...]


def _ml_scan_kernel(q_ref, k_ref, v_ref, gc_ref, gr_ref, o_ref, c_ref, n_ref, m_ref, *, n_ctx, n_all, scale):
    L = ML_CHUNK
    ii = lax.broadcasted_iota(jnp.int32, (L, L), 0)
    jj = lax.broadcasted_iota(jnp.int32, (L, L), 1)

    def run(d, rev):
        incl = (jj >= ii) if rev else (jj <= ii)
        incl_t = (ii >= jj) if rev else (ii <= jj)
        c_ref[...] = jnp.zeros_like(c_ref)
        n_ref[...] = jnp.zeros_like(n_ref)
        m_ref[...] = jnp.zeros_like(m_ref)

        def body(idx, carry):
            c = _chunk_of(idx, rev, n_ctx, n_all)
            rows = pl.ds(pl.multiple_of(c * L, L), L)
            qc = q_ref[0, rows, :]
            kc = k_ref[0, rows, :]
            vc = v_ref[0, rows, :]
            gcol = gc_ref[0, 0, rows, :]
            grow = gr_ref[0, 0, :, rows]
            i_col = gcol[:, 2 * d:2 * d + 1]
            f_col = jax.nn.log_sigmoid(gcol[:, 2 * d + 1:2 * d + 2])
            i_row = grow[2 * d:2 * d + 1, :]
            f_row = jax.nn.log_sigmoid(grow[2 * d + 1:2 * d + 2, :])
            b_col = jnp.sum(jnp.where(incl, f_row, 0.0), axis=1, keepdims=True)
            b_row = jnp.sum(jnp.where(incl_t, f_col, 0.0), axis=0, keepdims=True)
            b_all = jnp.sum(f_row, axis=1, keepdims=True)
            m_prev = m_ref[...]
            a_col = b_col + m_prev
            dlog = jnp.where(incl, b_col - b_row + i_row, -jnp.inf)
            m_t = jnp.maximum(a_col, jnp.max(dlog, axis=1, keepdims=True))
            s = _dot_nt(qc, kc) * (scale * jnp.exp(dlog - m_t))
            inter = jnp.exp(a_col - m_t)
            num = _dot(s.astype(BF16), vc) + inter * _dot(qc, c_ref[...].astype(BF16))
            den = jnp.sum(s, axis=1, keepdims=True) + inter * jnp.sum(
                qc.astype(F32) * n_ref[...], axis=1, keepdims=True)
            hout = num / jnp.maximum(jnp.abs(den), jnp.exp(-m_t))
            wlog = b_all - b_col + i_col
            m_new = jnp.maximum(b_all + m_prev, jnp.max(wlog, axis=0, keepdims=True))
            wk = kc.astype(F32) * (scale * jnp.exp(wlog - m_new))
            decay = jnp.exp(b_all + m_prev - m_new)
            c_ref[...] = decay * c_ref[...] + _dot_tn(wk.astype(BF16), vc)
            n_ref[...] = decay * n_ref[...] + jnp.sum(wk, axis=0, keepdims=True)
            m_ref[...] = m_new
            if rev:
                o = hout + o_ref[0, rows, :]
                mu = jnp.mean(o, -1, keepdims=True)
                oc = o - mu
                var = jnp.mean(oc * oc, -1, keepdims=True)
                hout = oc * lax.rsqrt(var + LN_EPS)
            o_ref[0, rows, :] = hout
            return carry

        lax.fori_loop(0, n_all, body, 0)

    run(0, False)
    run(1, True)


def _block_diag_weights(w):
    per = LANES // QKV_BLOCK
    groups = w.shape[1] // per
    w5 = w.reshape(3, groups, per, QKV_BLOCK, QKV_BLOCK)
    eye = jnp.eye(per, dtype=w.dtype)
    bd = jnp.einsum("sgncd,nm->sgncmd", w5, eye)
    return bd.reshape(3, groups, LANES, LANES)


def _mlstm_layer(xs, msel, ln_g, ln_b, w_in, conv_w, conv_b, w_qkv, gate_w, gate_b, skip, gn_g, w_out, alpha,
                 ctx_len):
    bsz, tt, d = xs.shape
    n_tiles = tt // ROW_TILE
    inner = w_out.shape[0]
    heads = MLSTM_HEADS
    dh = inner // heads
    halo = 8
    hpt = ROW_TILE // halo
    gw = gate_w.reshape(2, 3 * inner, 2, heads).transpose(1, 3, 0, 2).reshape(3 * inner, 4 * heads)
    gb = gate_b.reshape(2, 2, heads).transpose(2, 0, 1).reshape(1, 4 * heads)
    tile = lambda width: pl.BlockSpec((1, ROW_TILE, width), lambda b, t: (b, t, 0))
    qkv, xc, z, gc, gr = pl.pallas_call(
        functools.partial(_ml_in_kernel, inner=inner, n_tiles=n_tiles),
        grid=(bsz, n_tiles),
        in_specs=[
            tile(d),
            pl.BlockSpec((1, halo, d), lambda b, t: (b, jnp.maximum(t * hpt - 1, 0), 0)),
            pl.BlockSpec((1, halo, d), lambda b, t: (b, jnp.minimum((t + 1) * hpt, tt // halo - 1), 0)),
            pl.BlockSpec((1, 1, 3, d), lambda b, t: (b, jnp.minimum(t, 1), 0, 0)),
            _const_spec(w_in.shape),
            _const_spec(conv_w.shape),
            _const_spec((1, inner)),
            _const_spec((3, inner // LANES, LANES, LANES)),
            _const_spec((3 * inner, 4 * heads)),
            _const_spec((4 * heads, 3 * inner)),
            _const_spec((1, 4 * heads)),
            _const_spec((4 * heads, 1)),
        ],
        out_specs=[tile(3 * inner), tile(inner), tile(inner), tile(4 * heads),
                   pl.BlockSpec((1, 4 * heads, ROW_TILE), lambda b, t: (b, 0, t))],
        out_shape=[jax.ShapeDtypeStruct((bsz, tt, 3 * inner), BF16),
                   jax.ShapeDtypeStruct((bsz, tt, inner), F32),
                   jax.ShapeDtypeStruct((bsz, tt, inner), F32),
                   jax.ShapeDtypeStruct((bsz, tt, 4 * heads), F32),
                   jax.ShapeDtypeStruct((bsz, 4 * heads, tt), F32)],
        compiler_params=_cparams(2),
        name="ml_in",
    )(xs, xs, xs, msel, w_in.astype(BF16), conv_w, conv_b.reshape(1, inner),
      _block_diag_weights(w_qkv).astype(BF16), gw.astype(BF16), gw.T.astype(BF16), gb, gb.T)

    gcol = gc.reshape(bsz, tt, heads, 4).transpose(0, 2, 1, 3)
    grow = gr.reshape(bsz, heads, 4, tt)
    o = pl.pallas_call(
        functools.partial(_ml_scan_kernel, n_ctx=ctx_len // ML_CHUNK, n_all=tt // ML_CHUNK, scale=dh ** -0.5),
        grid=(bsz, heads),
        in_specs=[
            pl.BlockSpec((1, tt, dh), lambda b, h: (b, 0, h)),
            pl.BlockSpec((1, tt, dh), lambda b, h: (b, 0, heads + h)),
            pl.BlockSpec((1, tt, dh), lambda b, h: (b, 0, 2 * heads + h)),
            pl.BlockSpec((1, 1, tt, 4), lambda b, h: (b, h, 0, 0)),
            pl.BlockSpec((1, 1, 4, tt), lambda b, h: (b, h, 0, 0)),
        ],
        out_specs=pl.BlockSpec((1, tt, dh), lambda b, h: (b, 0, h)),
        out_shape=jax.ShapeDtypeStruct((bsz, tt, inner), F32),
        scratch_shapes=[pltpu.VMEM((dh, dh), F32), pltpu.VMEM((1, dh), F32), pltpu.VMEM((1, 1), F32)],
        compiler_params=_cparams(2),
        name="ml_scan",
    )(qkv, qkv, qkv, gcol, grow)

    return _out_call("ml", [o, xc, z], [gn_g, skip], w_out, xs, msel, ln_g, ln_b, alpha)


def _rk_in_kernel(x_ref, xp_ref, xn_ref, ms_ref, mix_ref, w4_ref, w1_ref, a1_ref, w2_ref, a2_ref, w0_ref, a0_ref,
                  r_ref, k_ref, v_ref, g_ref, lw0_ref, lw1_ref, ad0_ref, ad1_ref, sh_ref, *, n_tiles):
    t = pl.program_id(1)
    tm, d = x_ref.shape[1], x_ref.shape[2]
    ms = ms_ref[0, 0]
    u = _modulate(x_ref[0], ms)
    row = lax.broadcasted_iota(jnp.int32, (tm, 1), 0)
    dn = pltpu.roll(u, 1, 0)
    up = pltpu.roll(u, tm - 1, 0)

    @pl.when(t == 0)
    def _():
        hd = d // 2
        sh_ref[:, :hd] = jnp.where(row == 0, 0.0, dn[:, :hd])
        sh_ref[:, hd:] = jnp.where(row == tm - 1, 0.0, up[:, hd:])

    @pl.when(t > 0)
    def _():
        q = d // 4
        col = row % GRID_W
        sh_ref[:, :q] = jnp.where(col == 0, 0.0, dn[:, :q])
        sh_ref[:, q:2 * q] = jnp.where(col == GRID_W - 1, 0.0, up[:, q:2 * q])
        above = _modulate(xp_ref[0], ms)[:, 2 * q:3 * q] * (t >= 2).astype(F32)
        below = _modulate(xn_ref[0], ms)[:, 3 * q:] * (t < n_tiles - 1).astype(F32)
        sh_ref[:GRID_W, 2 * q:3 * q] = above
        sh_ref[GRID_W:, 2 * q:3 * q] = u[:tm - GRID_W, 2 * q:3 * q]
        sh_ref[:tm - GRID_W, 3 * q:] = u[GRID_W:, 3 * q:]
        sh_ref[tm - GRID_W:, 3 * q:] = below

    xx = sh_ref[...] - u
    mix = mix_ref[...]
    mixed = lambda j: u + xx * mix[j:j + 1]
    r_ref[0] = _dot(mixed(0).astype(BF16), w4_ref[0])
    k_ref[0] = _dot(mixed(2).astype(BF16), w4_ref[1])
    v_ref[0] = _dot(mixed(3).astype(BF16), w4_ref[2])
    g_ref[0] = _dot(mixed(5).astype(BF16), w4_ref[3])
    tw = jnp.tanh(_dot(mixed(1).astype(BF16), w1_ref[...]))
    la = _dot(mixed(4).astype(BF16), a1_ref[...])
    lane = lax.broadcasted_iota(jnp.int32, (1, tw.shape[1]), 1)
    half = tw.shape[1] // 2
    for dd, (lw_ref, ad_ref) in enumerate(((lw0_ref, ad0_ref), (lw1_ref, ad1_ref))):
        sel = jnp.logical_and(lane >= dd * half, lane < (dd + 1) * half)
        wl = w0_ref[dd:dd + 1, :] + _dot(jnp.where(sel, tw, 0.0).astype(BF16), w2_ref[...])
        wlog = -jax.nn.softplus(-wl) - 0.5
        lw_ref[0] = -jnp.exp(wlog)
        al = a0_ref[dd:dd + 1, :] + _dot(jnp.where(sel, la, 0.0).astype(BF16), a2_ref[...])
        ad_ref[0] = jax.nn.sigmoid(al)


def _rk_scan_kernel(r_ref, k_ref, v_ref, lw0_ref, lw1_ref, ad0_ref, ad1_ref, kk_ref, ka_ref, rk_ref, gng_ref,
                    gnb_ref, o_ref, s_ref, *, n_ctx, n_all):
    L = RK_CHUNK
    W = RK_PAIR * L
    lane = lax.broadcasted_iota(jnp.int32, (1, W), 1)
    head_masks = [jnp.logical_and(lane >= p * RWKV_N, lane < (p + 1) * RWKV_N).astype(F32) for p in range(RK_PAIR)]
    ri = lax.broadcasted_iota(jnp.int32, (W, W), 0)
    ci = lax.broadcasted_iota(jnp.int32, (W, W), 1)
    same = (ri // L) == (ci // L)
    eye = (ri == ci).astype(F32)
    ti = lax.broadcasted_iota(jnp.int32, (L, L), 0)
    tj = lax.broadcasted_iota(jnp.int32, (L, L), 1)
    kkp = kk_ref[...]
    kap = ka_ref[...]

    def stack(x):
        return jnp.concatenate([x * hm for hm in head_masks], axis=0)

    def head_sum(x):
        out = 0.0
        for hm in head_masks:
            out = out + hm * jnp.sum(x * hm, axis=1, keepdims=True)
        return out

    def kd_of(k, ad):
        return k * (1.0 + (ad - 1.0) * kap)

    def run(rev):
        lw_ref, ad_ref = (lw1_ref, ad1_ref) if rev else (lw0_ref, ad0_ref)
        strict = jnp.logical_and(same, (ci > ri) if rev else (ci < ri))
        incl = jnp.logical_and(same, (ci >= ri) if rev else (ci <= ri))
        tri = ((tj >= ti) if rev else (tj <= ti)).astype(BF16)
        s_ref[...] = jnp.zeros_like(s_ref)

        def body(idx, carry):
            c = _chunk_of(idx, rev, n_ctx, n_all)
            rows = pl.ds(pl.multiple_of(c * L, L), L)
            r = r_ref[0, rows, :]
            k = k_ref[0, rows, :]
            v = v_ref[0, rows, :]
            lw = lw_ref[0, rows, :]
            ad = ad_ref[0, rows, :]
            kk = k * kkp
            kk = kk / jnp.maximum(jnp.sqrt(head_sum(kk * kk)), 1e-12)
            b = kk * ad
            kd = kd_of(k, ad)
            l1 = lw.astype(BF16)
            rem = lw - l1.astype(F32)
            l2 = rem.astype(BF16)
            l3 = (rem - l2.astype(F32)).astype(BF16)
            cin = _dot(tri, l1) + (_dot(tri, l2) + _dot(tri, l3))
            cex = cin - lw
            ctot = cin[0:1, :] if rev else cin[L - 1:L, :]
            e_in = jnp.exp(cin)
            e_ni = jnp.exp(-cin)
            e_rem = jnp.exp(ctot - cin)
            a_t = -kk * jnp.exp(cex)
            lhs = jnp.concatenate([stack(a_t), stack(r * e_in)], axis=0).astype(BF16)
            rhs = jnp.concatenate([stack(b * e_ni), stack(kd * e_ni)], axis=0).astype(BF16)
            gram = _dot_nt(lhs, rhs)
            a_ab = jnp.where(strict, gram[:W, :W], 0.0)
            a_ak = jnp.where(strict, gram[:W, W:], 0.0)
            c_rb = jnp.where(incl, gram[W:, :W], 0.0)
            c_rk = jnp.where(incl, gram[W:, W:], 0.0)
            tinv = eye + a_ab
            pw = a_ab
            for _ in range(L.bit_length() - 2):
                pw = _dot_hi(pw, pw)
                tinv = tinv + _dot_hi(tinv, pw)
            v2 = stack(v).astype(BF16)
            xs = _dot_nt(lhs, s_ref[...].astype(BF16))
            x2 = xs[:W] + _dot(a_ak.astype(BF16), v2)
            u2 = _dot_hi(tinv, x2)
            u2b = u2.astype(BF16)
            y2 = xs[W:] + _dot(c_rb.astype(BF16), u2b) + _dot(c_rk.astype(BF16), v2)
            y = y2[:L]
            for p in range(1, RK_PAIR):
                y = y + y2[p * L:(p + 1) * L]
            s_ref[...] = (s_ref[...] * jnp.exp(ctot)
                          + _dot_tn(u2b, stack(b * e_rem).astype(BF16))
                          + _dot_tn(v2, stack(kd * e_rem).astype(BF16)))
            if rev:
                y = y + o_ref[0, rows, :]
                mu = head_sum(y) * (1.0 / RWKV_N)
                yc = y - mu
                var = head_sum(yc * yc) * (1.0 / RWKV_N)
                o = yc * lax.rsqrt(var + RWKV_GN_EPS) * gng_ref[...] + gnb_ref[...]
                rr = r * rk_ref[...]
                bonus = head_sum(rr * kd_of(k, ad0_ref[0, rows, :])) * v + head_sum(rr * kd) * v
                y = o + bonus
            o_ref[0, rows, :] = y
            return carry

        lax.fori_loop(0, n_all, body, 0)

    run(False)
    run(True)


def _rwkv_layer(xs, msel, ln_g, ln_b, mix, w_rkvg, w0, w1, w2, a0, a1, a2, k_k, k_a, r_k, gn_g, gn_b, w_out, alpha,
                ctx_len):
    bsz, tt, d = xs.shape
    n_tiles = tt // ROW_TILE
    lora = w1.shape[-1]
    hpt = ROW_TILE // GRID_W
    tile = lambda width: pl.BlockSpec((1, ROW_TILE, width), lambda b, t: (b, t, 0))
    cat_dirs_in = lambda w: jnp.concatenate([w[0], w[1]], axis=1).astype(BF16)
    cat_dirs_out = lambda w: jnp.concatenate([w[0], w[1]], axis=0).astype(BF16)
    outs = pl.pallas_call(
        functools.partial(_rk_in_kernel, n_tiles=n_tiles),
        grid=(bsz, n_tiles),
        in_specs=[
            tile(d),
            pl.BlockSpec((1, GRID_W, d), lambda b, t: (b, jnp.maximum(t * hpt - 1, 0), 0)),
            pl.BlockSpec((1, GRID_W, d), lambda b, t: (b, jnp.minimum((t + 1) * hpt, tt // GRID_W - 1), 0)),
            pl.BlockSpec((1, 1, 3, d), lambda b, t: (b, jnp.minimum(t, 1), 0, 0)),
            _const_spec(mix.shape),
            _const_spec(w_rkvg.shape),
            _const_spec((d, 2 * lora)),
            _const_spec((d, 2 * lora)),
            _const_spec((2 * lora, d)),
            _const_spec((2 * lora, d)),
            _const_spec((2, d)),
            _const_spec((2, d)),
        ],
        out_specs=[tile(d)] * 8,
        out_shape=[jax.ShapeDtypeStruct((bsz, tt, d), F32)] * 8,
        scratch_shapes=[pltpu.VMEM((ROW_TILE, d), F32)],
        compiler_params=_cparams(2),
        name="rk_in",
    )(xs, xs, xs, msel, mix, w_rkvg.astype(BF16), cat_dirs_in(w1), cat_dirs_in(a1), cat_dirs_out(w2),
      cat_dirs_out(a2), w0, a0)
    r, k, v, g, lw0, lw1, ad0, ad1 = outs

    width = RK_PAIR * RWKV_N
    seq = pl.BlockSpec((1, tt, width), lambda b, p: (b, 0, p))
    vec = pl.BlockSpec((1, width), lambda b, p: (0, p))
    o = pl.pallas_call(
        functools.partial(_rk_scan_kernel, n_ctx=ctx_len // RK_CHUNK, n_all=tt // RK_CHUNK),
        grid=(bsz, d // width),
        in_specs=[seq] * 7 + [vec] * 5,
        out_specs=seq,
        out_shape=jax.ShapeDtypeStruct((bsz, tt, d), F32),
        scratch_shapes=[pltpu.VMEM((width, width), F32)],
        compiler_params=_cparams(2),
        name="rk_scan",
    )(r, k, v, lw0, lw1, ad0, ad1, k_k.reshape(1, d), k_a.reshape(1, d), r_k.reshape(1, d), gn_g.reshape(1, d),
      gn_b.reshape(1, d))

    return _out_call("rk", [o, g], [], w_out, xs, msel, ln_g, ln_b, alpha)


def _rope_tables(ctx_len, seq, dk):
    half = dk // 2
    nf = half // 2
    inv = ROPE_BASE ** (-jnp.arange(nf, dtype=F32) / nf)
    t = jnp.arange(seq)
    rows = (t // GRID_W).astype(F32)
    cols = (t % GRID_W).astype(F32)
    ar = rows[:, None] * inv[None, :]
    ac = cols[:, None] * inv[None, :]
    cos = jnp.concatenate([jnp.cos(ar), jnp.cos(ar), jnp.cos(ac), jnp.cos(ac)], -1)
    sin = jnp.concatenate([-jnp.sin(ar), jnp.sin(ar), -jnp.sin(ac), jnp.sin(ac)], -1)
    cos = jnp.concatenate([jnp.ones((ctx_len, dk), F32), cos], 0)
    sin = jnp.concatenate([jnp.zeros((ctx_len, dk), F32), sin], 0)
    return cos, sin


def kernel(x, c, ctx, c_ctx, ada_w, ada_b, ln_g, ln_b, ret_w_in, ret_decay, ret_w_out, ml_w_in, ml_conv_w,
           ml_conv_b, ml_w_qkv, ml_gate_w, ml_gate_b, ml_skip, ml_gn_g, ml_w_out, rk_mix, rk_w_rkvg, rk_w0, rk_w1,
           rk_w2, rk_a0, rk_a1, rk_a2, rk_k_k, rk_k_a, rk_r_k, rk_gn_g, rk_gn_b, rk_w_out):
    bsz, seq, d = x.shape
    ctx_len = ctx.shape[1]
    depth = ada_w.shape[0]
    assert ctx_len == ROW_TILE and seq % ROW_TILE == 0 and seq % GRID_W == 0
    alpha = (2.0 * depth) ** 0.25

    pad = (-(bsz + 1)) % 8
    c_all = jnp.concatenate([c, c_ctx[None, :], jnp.zeros((pad, d), F32)], axis=0)
    mods = _ada_mods(c_all, ada_w, ada_b)
    lat = mods[:, :bsz].reshape(depth, bsz, 3, d)
    cmod = jnp.broadcast_to(mods[:, bsz].reshape(depth, 1, 3, d), (depth, bsz, 3, d))
    msel_all = jnp.stack([cmod, lat], axis=2)

    rope_cos, rope_sin = _rope_tables(ctx_len, seq, d // RET_HEADS)
    xs = jnp.concatenate([ctx, x], axis=1)
    for i in range(depth):
        kind, j = i % 3, i // 3
        msel = msel_all[i]
        if kind == 0:
            xs = _retention_layer(xs, msel, ln_g[i], ln_b[i], ret_w_in[j], ret_decay[j], ret_w_out[j],
                                  rope_cos, rope_sin, alpha, ctx_len)
        elif kind == 1:
            xs = _mlstm_layer(xs, msel, ln_g[i], ln_b[i], ml_w_in[j], ml_conv_w[j], ml_conv_b[j], ml_w_qkv[j],
                              ml_gate_w[j], ml_gate_b[j], ml_skip[j], ml_gn_g[j], ml_w_out[j], alpha, ctx_len)
        else:
            xs = _rwkv_layer(xs, msel, ln_g[i], ln_b[i], rk_mix[j], rk_w_rkvg[j], rk_w0[j], rk_w1[j], rk_w2[j],
                             rk_a0[j], rk_a1[j], rk_a2[j], rk_k_k[j], rk_k_a[j], rk_r_k[j], rk_gn_g[j],
                             rk_gn_b[j], rk_w_out[j], alpha, ctx_len)
    return xs[:, ctx_len:]
```
